```python
import jax
import jax.numpy as jnp
from jax import lax
import numpy as np

D_MODEL = 1024
BATCH = 8
SEQ = 2048
DEPTH = 4
DEC_BATCH = 32
DEC_SEQ = 1
PAST_LEN = 16384
PAGE_SIZE = 128

N_MIXERS = 2
N_RWKV = (DEPTH + 1) // 2
N_MLA = DEPTH // 2
RW_HEAD = 64
RW_HEADS = D_MODEL // RW_HEAD
RW_DECAY_LORA = 64
RW_AAA_LORA = 64
RW_MV_LORA = 32
RW_GATE_LORA = 128
RW_GN_EPS = 64e-5
MLA_HEADS = 16
MLA_Q_LORA = 384
MLA_KV_LORA = 256
MLA_NOPE = 64
MLA_ROPE = 32
MLA_V = 64
MLA_SCALE = (MLA_NOPE + MLA_ROPE) ** -0.5
ROPE_THETA = 10000.0
Q_BLOCK = 128
PEER_HEADS = 8
PEER_N_KEYS = 128
PEER_N_EXPERTS = PEER_N_KEYS * PEER_N_KEYS
PEER_TOPK = 16
PEER_D_KEY = 256
PEER_BLOCK = 128
NORM_EPS = 1e-6

kernel_name = 'rwkv7_mla_peer_hybrid_step'

F32 = jnp.float32


def rms_norm(x, g):
    xf = x.astype(F32)
    y = xf * lax.rsqrt(jnp.mean(xf * xf, axis=-1, keepdims=True) + NORM_EPS)
    return y.astype(x.dtype) * g


def head_group_norm(y, w, b):
    B, T = y.shape[:2]
    yf = y.astype(F32)
    yc = yf - jnp.mean(yf, axis=-1, keepdims=True)
    var = jnp.mean(yc * yc, axis=-1, keepdims=True)
    yn = (yc * lax.rsqrt(var + RW_GN_EPS)).astype(y.dtype).reshape(B, T, D_MODEL)
    return yn * w + b


def wkv7_scan(r, decay, k, v, kk, a, s0):
    xs = tuple(jnp.moveaxis(t.astype(F32), 1, 0) for t in (r, decay, k, v, kk, a))

    def step(s, inp):
        r_t, w_t, k_t, v_t, kk_t, a_t = inp
        s_kk = jnp.einsum('bhvk,bhk->bhv', s, kk_t)
        s = (s * w_t[:, :, None, :]
             - s_kk[..., None] * (kk_t * a_t)[:, :, None, :]
             + v_t[..., None] * k_t[:, :, None, :])
        return s, jnp.einsum('bhvk,bhk->bhv', s, r_t)

    s, ys = lax.scan(step, s0.astype(F32), xs)
    return jnp.moveaxis(ys, 0, 1).astype(r.dtype), s.astype(s0.dtype)


def rwkv7_mix(h, shift_prev, wkv0, v_first, vres, mu, w_in, w0, w1, w2, a0, a1, a2,
              g1, g2, k_k, k_a, r_k, ln_w, ln_b, w_out):
    B, T, _ = h.shape
    prev = jnp.concatenate([shift_prev[:, None, :].astype(h.dtype), h[:, :-1]], axis=1)
    xx = prev - h
    xr, xw, xk, xv, xa, xg = [h + xx * mu[m] for m in range(6)]
    r, k, v = jnp.einsum('gbtd,gde->gbte', jnp.stack([xr, xk, xv]), w_in)
    w_log = -jax.nn.softplus(-(w0 + jnp.tanh(xw @ w1) @ w2).astype(F32)) - 0.5
    decay = jnp.exp(-jnp.exp(w_log))
    a = jax.nn.sigmoid(a0 + (xa @ a1) @ a2)
    g = jax.nn.sigmoid(xg @ g1) @ g2
    if vres is not None:
        v0, v1, v2 = vres
        v = v + (v_first - v) * jax.nn.sigmoid(v0 + (xv @ v1) @ v2)

    def heads(t):
        return t.reshape(B, T, RW_HEADS, RW_HEAD)

    kk = heads(k * k_k).astype(F32)
    kk = kk * lax.rsqrt(jnp.maximum(jnp.sum(kk * kk, axis=-1, keepdims=True), 1e-24))
    k = k * (1.0 + (a - 1.0) * k_a)
    r_h, k_h, v_h = heads(r), heads(k), heads(v)
    y, wkv = wkv7_scan(r_h, heads(decay), k_h, v_h, kk, heads(a), wkv0)
    y = head_group_norm(y, ln_w, ln_b)
    bonus = jnp.sum(r_h * k_h * r_k, axis=-1, keepdims=True) * v_h
    y = y + bonus.reshape(B, T, D_MODEL)
    return (y * g) @ w_out, v, h[:, -1], wkv


def rope_tables(pos):
    inv = ROPE_THETA ** (-jnp.arange(0, MLA_ROPE, 2, dtype=F32) / MLA_ROPE)
    ang = pos.astype(F32)[:, None] * inv[None, :]
    return jnp.cos(ang), jnp.sin(ang)


def apply_rope(x, cos, sin):
    half = x.shape[-1] // 2
    x1 = x[..., :half].astype(F32)
    x2 = x[..., half:].astype(F32)
    return jnp.concatenate([x1 * cos - x2 * sin, x2 * cos + x1 * sin], axis=-1).astype(x.dtype)


def split_kvb(w_kvb):
    w = w_kvb.reshape(MLA_KV_LORA, MLA_HEADS, MLA_NOPE + MLA_V)
    return w[..., :MLA_NOPE], w[..., MLA_NOPE:]


def mla_keys(ckv, w_uk, kn_nope):
    return rms_norm(jnp.einsum('...c,chd->...hd', ckv, w_uk), kn_nope)


def mla_scores(q_nope, q_pe, k_nope, k_pe):
    s = (jnp.einsum('bqhd,bkhd->bhqk', q_nope, k_nope)
         + jnp.einsum('bqhr,bkr->bhqk', q_pe, k_pe))
    return s.astype(F32) * MLA_SCALE


def mla_prompt_attention(q_nope, q_pe, k_nope, k_pe, v):
    B, S = q_nope.shape[:2]
    k_pos = jnp.arange(S)

    def block(start):
        qn = lax.dynamic_slice_in_dim(q_nope, start, Q_BLOCK, axis=1)
        qp = lax.dynamic_slice_in_dim(q_pe, start, Q_BLOCK, axis=1)
        s = mla_scores(qn, qp, k_nope, k_pe)
        q_pos = start + jnp.arange(Q_BLOCK)
        s = jnp.where(k_pos[None, :] <= q_pos[:, None], s, -jnp.inf)
        p = jax.nn.softmax(s, axis=-1)
        return jnp.einsum('bhqk,bkhv->bqhv', p.astype(v.dtype), v)

    o = lax.map(block, jnp.arange(0, S, Q_BLOCK))
    return jnp.moveaxis(o, 0, 1).reshape(B, S, MLA_HEADS, MLA_V)


def mla_sample_attention(q_nope, q_pe, ckv_new, k_nope_new, k_pe_new, pool_ckv, pool_kpe,
                         page_table, w_uk, w_uv, kn_nope):
    B, T = q_nope.shape[:2]

    def merge(carry, s, ckv):
        m, l, acc = carry
        m_new = jnp.maximum(m, jnp.max(s, axis=-1))
        corr = jnp.exp(m - m_new)
        p = jnp.exp(s - m_new[..., None])
        l = l * corr + jnp.sum(p, axis=-1)
        acc = acc * corr[..., None] + jnp.einsum('bhqk,bkc->bhqc', p, ckv.astype(F32))
        return (m_new, l, acc)

    def page_step(carry, p_idx):
        phys = page_table[:, p_idx]
        ckv = pool_ckv[phys]
        s = mla_scores(q_nope, q_pe, mla_keys(ckv, w_uk, kn_nope), pool_kpe[phys])
        return merge(carry, s, ckv), None

    init = (jnp.full((B, MLA_HEADS, T), -jnp.inf, F32),
            jnp.zeros((B, MLA_HEADS, T), F32),
            jnp.zeros((B, MLA_HEADS, T, MLA_KV_LORA), F32))
    carry, _ = lax.scan(page_step, init, jnp.arange(page_table.shape[1]))
    s = mla_scores(q_nope, q_pe, k_nope_new, k_pe_new)
    s = jnp.where(jnp.tril(jnp.ones((T, T), dtype=bool)), s, -jnp.inf)
    _, l, acc = merge(carry, s, ckv_new)
    o = jnp.einsum('bhqc,chv->bqhv', acc / l[..., None], w_uv.astype(F32))
    return o.astype(q_nope.dtype)


def mla_mix(h, cos, sin, past, w_in, q_norm, kv_norm, w_qb, w_kvb, qn_nope, qn_rope,
            kn_nope, kn_rope, w_out):
    B, T, _ = h.shape
    proj = h @ w_in
    cq = rms_norm(proj[..., :MLA_Q_LORA], q_norm)
    ckv = rms_norm(proj[..., MLA_Q_LORA:MLA_Q_LORA + MLA_KV_LORA], kv_norm)
    k_pe = apply_rope(rms_norm(proj[..., MLA_Q_LORA + MLA_KV_LORA:], kn_rope), cos, sin)
    q = (cq @ w_qb).reshape(B, T, MLA_HEADS, MLA_NOPE + MLA_ROPE)
    q_nope = rms_norm(q[..., :MLA_NOPE], qn_nope)
    q_pe = apply_rope(rms_norm(q[..., MLA_NOPE:], qn_rope), cos[:, None], sin[:, None])
    w_uk, w_uv = split_kvb(w_kvb)
    k_nope = mla_keys(ckv, w_uk, kn_nope)
    if past is None:
        v = jnp.einsum('btc,chv->bthv', ckv, w_uv)
        o = mla_prompt_attention(q_nope, q_pe, k_nope, k_pe, v)
    else:
        pool_ckv, pool_kpe, page_table = past
        o = mla_sample_attention(q_nope, q_pe, ckv, k_nope, k_pe, pool_ckv, pool_kpe,
                                 page_table, w_uk, w_uv, kn_nope)
    return o.reshape(B, T, MLA_HEADS * MLA_V) @ w_out, ckv, k_pe


def peer_ffn(x, w_q, sub_keys, u_tab, v_tab):
    shape = x.shape
    xt = x.reshape(-1, D_MODEL)
    n = xt.shape[0]
    n_pad = (-n) % PEER_BLOCK
    blocks = jnp.pad(xt, ((0, n_pad), (0, 0))).reshape(-1, PEER_BLOCK, D_MODEL)
    n_cand = PEER_TOPK * PEER_TOPK

    def block(xb):
        q = (xb @ w_q).reshape(PEER_BLOCK, PEER_HEADS, 2, PEER_D_KEY // 2)
        s = jnp.einsum('thcd,cnd->thcn', q, sub_keys).astype(F32)
        s1, i1 = lax.top_k(s[:, :, 0], PEER_TOPK)
        s2, i2 = lax.top_k(s[:, :, 1], PEER_TOPK)
        cand_s = (s1[..., :, None] + s2[..., None, :]).reshape(PEER_BLOCK, PEER_HEADS, n_cand)
        cand_e = (i1[..., :, None] * PEER_N_KEYS + i2[..., None, :]).reshape(PEER_BLOCK, PEER_HEADS, n_cand)
        top_s, top_pos = lax.top_k(cand_s, PEER_TOPK)
        experts = jnp.take_along_axis(cand_e, top_pos, axis=-1)
        gate = jax.nn.softmax(top_s, axis=-1)
        act = jax.nn.gelu(jnp.einsum('thkd,td->thk', u_tab[experts], xb).astype(F32))
        return jnp.einsum('thk,thkd->td', (gate * act).astype(xb.dtype), v_tab[experts])

    y = lax.map(block, blocks).reshape(-1, D_MODEL)[:n]
    return y.reshape(shape)


def setup_inputs(seed: int = 0) -> dict:
    key = jax.random.key(seed)
    keys = iter(jax.random.split(key, 64))

    def nrm(shape, scale):
        return jax.random.normal(next(keys), shape, F32) * scale

    def gain(shape):
        return 1.0 + nrm(shape, 0.02)

    D = D_MODEL
    n_pages = PAST_LEN // PAGE_SIZE
    n_used = DEC_BATCH * n_pages
    n_pool = (5 * n_used + 3) // 4
    page_table = jax.random.permutation(next(keys), n_pool)[:n_used].reshape(DEC_BATCH, n_pages).astype(jnp.int32)
    nv = max(N_RWKV - 1, 0)
    return {
        'x_prompt': nrm((BATCH, SEQ, D), 1.0),
        'x_sample': nrm((DEC_BATCH, DEC_SEQ, D), 1.0),
        'cache_ckv': nrm((N_MLA, n_pool, PAGE_SIZE, MLA_KV_LORA), 1.0),
        'cache_kpe': nrm((N_MLA, n_pool, PAGE_SIZE, MLA_ROPE), 1.0),
        'page_table': page_table,
        'state_wkv': nrm((N_RWKV, DEC_BATCH, RW_HEADS, RW_HEAD, RW_HEAD), 0.3),
        'state_shift': nrm((N_RWKV, DEC_BATCH, D), 1.0),
        'norm_mix': gain((DEPTH, D)),
        'norm_ffn': gain((DEPTH, D)),
        'rw_mu': jax.random.uniform(next(keys), (N_RWKV, 6, D), F32),
        'rw_w_in': nrm((N_RWKV, 3, D, D), D ** -0.5),
        'rw_w0': nrm((N_RWKV, D), 0.5),
        'rw_w1': nrm((N_RWKV, D, RW_DECAY_LORA), D ** -0.5),
        'rw_w2': nrm((N_RWKV, RW_DECAY_LORA, D), 0.5 * RW_DECAY_LORA ** -0.5),
        'rw_a0': nrm((N_RWKV, D), 0.5),
        'rw_a1': nrm((N_RWKV, D, RW_AAA_LORA), D ** -0.5),
        'rw_a2': nrm((N_RWKV, RW_AAA_LORA, D), RW_AAA_LORA ** -0.5),
        'rw_v0': nrm((nv, D), 0.5),
        'rw_v1': nrm((nv, D, RW_MV_LORA), D ** -0.5),
        'rw_v2': nrm((nv, RW_MV_LORA, D), RW_MV_LORA ** -0.5),
        'rw_g1': nrm((N_RWKV, D, RW_GATE_LORA), D ** -0.5),
        'rw_g2': nrm((N_RWKV, RW_GATE_LORA, D), RW_GATE_LORA ** -0.5),
        'rw_k_k': 1.0 + nrm((N_RWKV, D), 0.1),
        'rw_k_a': 1.0 + nrm((N_RWKV, D), 0.1),
        'rw_r_k': nrm((N_RWKV, RW_HEADS, RW_HEAD), 0.1),
        'rw_ln_w': gain((N_RWKV, D)),
        'rw_ln_b': nrm((N_RWKV, D), 0.02),
        'rw_w_out': nrm((N_RWKV, D, D), D ** -0.5),
        'mla_w_in': nrm((N_MLA, D, MLA_Q_LORA + MLA_KV_LORA + MLA_ROPE), D ** -0.5),
        'mla_q_norm': gain((N_MLA, MLA_Q_LORA)),
        'mla_kv_norm': gain((N_MLA, MLA_KV_LORA)),
        'mla_w_qb': nrm((N_MLA, MLA_Q_LORA, MLA_HEADS * (MLA_NOPE + MLA_ROPE)), MLA_Q_LORA ** -0.5),
        'mla_w_kvb': nrm((N_MLA, MLA_KV_LORA, MLA_HEADS * (MLA_NOPE + MLA_V)), MLA_KV_LORA ** -0.5),
        'mla_qn_nope': gain((N_MLA, MLA_NOPE)),
        'mla_qn_rope': gain((N_MLA, MLA_ROPE)),
        'mla_kn_nope': gain((N_MLA, MLA_NOPE)),
        'mla_kn_rope': gain((N_MLA, MLA_ROPE)),
        'mla_w_out': nrm((N_MLA, MLA_HEADS * MLA_V, D), (MLA_HEADS * MLA_V) ** -0.5),
        'peer_w_q': nrm((DEPTH, D, PEER_HEADS * PEER_D_KEY), D ** -0.5),
        'peer_sub_keys': nrm((DEPTH, 2, PEER_N_KEYS, PEER_D_KEY // 2), (PEER_D_KEY // 2) ** -0.5),
        'peer_u': nrm((DEPTH, PEER_N_EXPERTS, D), D ** -0.5),
        'peer_v': nrm((DEPTH, PEER_N_EXPERTS, D), 0.2),
    }


def reference(x_prompt, x_sample, cache_ckv, cache_kpe, page_table, state_wkv, state_shift,
              norm_mix, norm_ffn,
              rw_mu, rw_w_in, rw_w0, rw_w1, rw_w2, rw_a0, rw_a1, rw_a2, rw_v0, rw_v1, rw_v2,
              rw_g1, rw_g2, rw_k_k, rw_k_a, rw_r_k, rw_ln_w, rw_ln_b, rw_w_out,
              mla_w_in, mla_q_norm, mla_kv_norm, mla_w_qb, mla_w_kvb, mla_qn_nope, mla_qn_rope,
              mla_kn_nope, mla_kn_rope, mla_w_out,
              peer_w_q, peer_sub_keys, peer_u, peer_v):

    def trunk(x, pos, shift_init, wkv_init, past):
        cos, sin = rope_tables(pos)
        v_first = None
        shifts, wkvs, ckvs, kpes = [], [], [], []
        for i in range(DEPTH):
            j = i // N_MIXERS
            h = rms_norm(x, norm_mix[i])
            if i % N_MIXERS == 0:
                vres = None if j == 0 else (rw_v0[j - 1], rw_v1[j - 1], rw_v2[j - 1])
                out, v, s_last, wkv_last = rwkv7_mix(
                    h, shift_init[j], wkv_init[j], v_first, vres, rw_mu[j], rw_w_in[j],
                    rw_w0[j], rw_w1[j], rw_w2[j], rw_a0[j], rw_a1[j], rw_a2[j],
                    rw_g1[j], rw_g2[j], rw_k_k[j], rw_k_a[j], rw_r_k[j],
                    rw_ln_w[j], rw_ln_b[j], rw_w_out[j])
                if j == 0:
                    v_first = v
                shifts.append(s_last)
                wkvs.append(wkv_last)
            else:
                layer_past = None if past is None else (past[0][j], past[1][j], past[2])
                out, ckv, kpe = mla_mix(
                    h, cos, sin, layer_past, mla_w_in[j], mla_q_norm[j], mla_kv_norm[j],
                    mla_w_qb[j], mla_w_kvb[j], mla_qn_nope[j], mla_qn_rope[j],
                    mla_kn_nope[j], mla_kn_rope[j], mla_w_out[j])
                ckvs.append(ckv)
                kpes.append(kpe)
            x = x + out
            x = x + peer_ffn(rms_norm(x, norm_ffn[i]), peer_w_q[i], peer_sub_keys[i],
                             peer_u[i], peer_v[i])
        return x, jnp.stack(ckvs), jnp.stack(kpes), jnp.stack(wkvs), jnp.stack(shifts)

    b_p, s_p = x_prompt.shape[:2]
    shift0 = jnp.zeros((N_RWKV, b_p, D_MODEL), x_prompt.dtype)
    wkv0 = jnp.zeros((N_RWKV, b_p, RW_HEADS, RW_HEAD, RW_HEAD), state_wkv.dtype)
    y_prompt, ckv_prompt, kpe_prompt, wkv_prompt, shift_prompt = trunk(
        x_prompt, jnp.arange(s_p), shift0, wkv0, None)

    past_len = page_table.shape[1] * cache_ckv.shape[2]
    pos_sample = past_len + jnp.arange(x_sample.shape[1])
    y_sample, ckv_sample, kpe_sample, wkv_sample, shift_sample = trunk(
        x_sample, pos_sample, state_shift, state_wkv, (cache_ckv, cache_kpe, page_table))

    return (y_prompt, y_sample, ckv_prompt, kpe_prompt, wkv_prompt, shift_prompt,
            ckv_sample, kpe_sample, wkv_sample, shift_sample)
```

```python
import functools

import jax
import jax.numpy as jnp
from jax import lax
from jax.experimental import pallas as pl
from jax.experimental.pallas import tpu as pltpu

F32 = jnp.float32
BF16 = jnp.bfloat16

LANES = 128
SUBLANES = 8
VMEM_LIMIT_BYTES = 52 * 1024 * 1024

RW_HEAD = 64
RW_GN_EPS = 64e-5
MLA_NOPE = 64
MLA_ROPE = 32
MLA_V = 64
ROPE_THETA = 10000.0
NORM_EPS = 1e-6
PEER_TOPK = 16

TOKEN_TILE = 512
PEER_GROUP = 1024
WKV_TOKENS = 32
FLASH_BLOCK = 256
PAGES_PER_STEP = 4

_CAND_PAIRS = tuple((a, b) for a in range(PEER_TOPK) for b in range(PEER_TOPK)
                    if (a + 1) * (b + 1) <= PEER_TOPK)


def _round_up(n, m):
    return (n + m - 1) // m * m


def _params(*sem):
    return pltpu.CompilerParams(dimension_semantics=sem, vmem_limit_bytes=VMEM_LIMIT_BYTES)


def _mm_kernel(x_ref, w_ref, o_ref):
    o_ref[...] = jnp.dot(x_ref[...].astype(BF16), w_ref[...], preferred_element_type=F32)


def _mm(x, w):
    m, k = x.shape
    n = w.shape[1]
    tm = TOKEN_TILE if m % TOKEN_TILE == 0 else m
    return pl.pallas_call(
        _mm_kernel,
        out_shape=jax.ShapeDtypeStruct((m, n), F32),
        grid=(m // tm,),
        in_specs=[pl.BlockSpec((tm, k), lambda i: (i, 0)),
                  pl.BlockSpec((k, n), lambda i: (0, 0))],
        out_specs=pl.BlockSpec((tm, n), lambda i: (i, 0)),
        compiler_params=_params("arbitrary"),
        name="token_matmul",
    )(x, w.astype(BF16))


_NORM_ROWS = 32


def _rmsnorm_kernel(x_ref, g_ref, o_ref):
    g = g_ref[...]

    def body(i, c):
        rows = pl.ds(pl.multiple_of(i * _NORM_ROWS, _NORM_ROWS), _NORM_ROWS)
        x = x_ref[rows, :]
        ms = jnp.mean(x * x, axis=-1, keepdims=True)
        o_ref[rows, :] = (x * lax.rsqrt(ms + NORM_EPS)) * g
        return c

    lax.fori_loop(0, x_ref.shape[0] // _NORM_ROWS, body, 0)


def _rmsnorm(x, g):
    m, d = x.shape
    tm = TOKEN_TILE if m % TOKEN_TILE == 0 else m
    return pl.pallas_call(
        _rmsnorm_kernel,
        out_shape=jax.ShapeDtypeStruct((m, d), F32),
        grid=(m // tm,),
        in_specs=[pl.BlockSpec((tm, d), lambda i: (i, 0)),
                  pl.BlockSpec((1, d), lambda i: (0, 0))],
        out_specs=pl.BlockSpec((tm, d), lambda i: (i, 0)),
        compiler_params=_params("arbitrary"),
        name="rmsnorm",
    )(x, g.reshape(1, d))


def _sublane_allsum(x):
    x = x + pltpu.roll(x, 4, 0)
    x = x + pltpu.roll(x, 2, 0)
    return x + pltpu.roll(x, 1, 0)


def _wkv_kernel(r_ref, w_ref, k_ref, kk_ref, ka_ref, v_ref, s0_ref, y_ref, s_ref):
    @pl.when(pl.program_id(1) == 0)
    def _():
        s_ref[...] = s0_ref[...]

    sub = lax.broadcasted_iota(jnp.int32, (SUBLANES, LANES), 0)

    def step(t, c):
        for vg in range(RW_HEAD // SUBLANES):
            yv = jnp.zeros((SUBLANES, LANES), F32)
            for vi in range(SUBLANES):
                vidx = vg * SUBLANES + vi
                s = s_ref[vidx]
                skk = _sublane_allsum(jnp.sum(s * kk_ref[t], axis=0))
                vrow = v_ref[t, pl.ds(vidx, 1), :]
                s = s * w_ref[t] - skk[None] * ka_ref[t] + vrow[None] * k_ref[t]
                s_ref[vidx] = s
                y = _sublane_allsum(jnp.sum(s * r_ref[t], axis=0))
                yv = jnp.where(sub == vi, y, yv)
            y_ref[t, vg * SUBLANES:(vg + 1) * SUBLANES, :] = yv
        return c

    lax.fori_loop(0, r_ref.shape[0], step, 0)


def _wkv(r, w, k, kk, ka, v, s0):
    t, n, l = r.shape
    tb = WKV_TOKENS if t % WKV_TOKENS == 0 else t
    nb = n // SUBLANES
    key4 = lambda a: a.reshape(t, nb, SUBLANES, l)
    kspec = pl.BlockSpec((tb, nb, SUBLANES, LANES), lambda g, i: (i, 0, 0, g))
    vspec = pl.BlockSpec((tb, n, LANES), lambda g, i: (i, 0, g))
    sspec = pl.BlockSpec((n, nb, SUBLANES, LANES), lambda g, i: (0, 0, 0, g))
    y, s = pl.pallas_call(
        _wkv_kernel,
        out_shape=(jax.ShapeDtypeStruct((t, n, l), F32),
                   jax.ShapeDtypeStruct((n, nb, SUBLANES, l), F32)),
        grid=(l // LANES, t // tb),
        in_specs=[kspec, kspec, kspec, kspec, kspec, vspec, sspec],
        out_specs=(vspec, sspec),
        compiler_params=_params("arbitrary", "arbitrary"),
        name="wkv7_scan",
    )(key4(r), key4(w), key4(k), key4(kk), key4(ka), v, s0.reshape(n, nb, SUBLANES, l))
    return y, s.reshape(n, n, l)


def _flash_kernel(q_ref, k_ref, v_ref, o_ref, *, scale):
    qi = pl.program_id(2)
    tq = q_ref.shape[2]
    q = q_ref[0, 0]
    row = qi * tq + lax.broadcasted_iota(jnp.int32, (tq, tq), 0)
    col0 = lax.broadcasted_iota(jnp.int32, (tq, tq), 1)

    def body(j, carry):
        m, l, acc = carry
        rows = pl.ds(pl.multiple_of(j * tq, tq), tq)
        kj = k_ref[0, 0, rows, :]
        vj = v_ref[0, 0, rows, :]
        s = lax.dot_general(q, kj, (((1,), (1,)), ((), ())), preferred_element_type=F32) * scale
        s = jnp.where(col0 + j * tq <= row, s, -jnp.inf)
        m_new = jnp.maximum(m, jnp.max(s, axis=-1, keepdims=True))
        p = jnp.exp(s - m_new)
        corr = jnp.exp(m - m_new)
        l = l * corr + jnp.sum(p, axis=-1, keepdims=True)
        acc = acc * corr + jnp.dot(p.astype(BF16), vj, preferred_element_type=F32)
        return m_new, l, acc

    init = (jnp.full((tq, 1), -jnp.inf, F32), jnp.zeros((tq, 1), F32),
            jnp.zeros((tq, v_ref.shape[3]), F32))
    _, l, acc = lax.fori_loop(0, qi + 1, body, init)
    o_ref[0, 0] = acc / l


def _flash(q, k, v, scale):
    b, h, s, dq = q.shape
    dv = v.shape[3]
    tq = FLASH_BLOCK
    return pl.pallas_call(
        functools.partial(_flash_kernel, scale=scale),
        out_shape=jax.ShapeDtypeStruct((b, h, s, dv), F32),
        grid=(b, h, s // tq),
        in_specs=[pl.BlockSpec((1, 1, tq, dq), lambda bi, hi, qi: (bi, hi, qi, 0)),
                  pl.BlockSpec((1, 1, s, dq), lambda bi, hi, qi: (bi, hi, 0, 0)),
                  pl.BlockSpec((1, 1, s, dv), lambda bi, hi, qi: (bi, hi, 0, 0))],
        out_specs=pl.BlockSpec((1, 1, tq, dv), lambda bi, hi, qi: (bi, hi, qi, 0)),
        compiler_params=_params("arbitrary", "arbitrary", "arbitrary"),
        name="mla_prompt_attention",
    )(q, k, v)


def _split_dot_nt(ind_t, x):
    hi = x.astype(BF16)
    lo = (x - hi.astype(F32)).astype(BF16)
    dn = (((1,), (1,)), ((), ()))
    return (lax.dot_general(ind_t, hi, dn, preferred_element_type=F32)
            + lax.dot_general(ind_t, lo, dn, preferred_element_type=F32))


def _paged_kernel(pt_ref, *refs, scale, n_pages):
    del pt_ref
    ckv_refs = refs[:n_pages]
    kpe_refs = refs[n_pages:2 * n_pages]
    (wuk_ref, ind_ref, qn_ref, qpe_ref, snew_ref, cnew_ref, o_ref, m_ref, l_ref, acc_ref) = refs[2 * n_pages:]
    pi = pl.program_id(1)

    @pl.when(pi == 0)
    def _():
        m_ref[...] = jnp.full(m_ref.shape, -jnp.inf, F32)
        l_ref[...] = jnp.zeros(l_ref.shape, F32)
        acc_ref[...] = jnp.zeros(acc_ref.shape, F32)

    def merge(s, c):
        m = m_ref[...]
        m_new = jnp.maximum(m, jnp.max(s, axis=-1, keepdims=True))
        corr = jnp.exp(m - m_new)
        p = jnp.exp(s - m_new)
        l_ref[...] = l_ref[...] * corr + jnp.sum(p, axis=-1, keepdims=True)
        acc_ref[...] = acc_ref[...] * corr + jnp.dot(p.astype(BF16), c, preferred_element_type=F32)
        m_ref[...] = m_new

    ind = ind_ref[...]
    qn = qn_ref[0]
    qpe = qpe_ref[0].astype(BF16)
    n_feat = wuk_ref.shape[1] // ind.shape[0]
    for ckv_ref, kpe_ref in zip(ckv_refs, kpe_refs):
        c = ckv_ref[0].astype(BF16)
        kup = jnp.dot(c, wuk_ref[...], preferred_element_type=F32)
        ssq = _split_dot_nt(ind, kup * kup)
        num = _split_dot_nt(ind, kup * qn)
        s_pe = lax.dot_general(qpe, kpe_ref[0].astype(BF16), (((1,), (1,)), ((), ())),
                               preferred_element_type=F32)
        s = (num * lax.rsqrt(ssq * (1.0 / n_feat) + NORM_EPS) + s_pe) * scale
        merge(s, c)

    @pl.when(pi == pl.num_programs(1) - 1)
    def _():
        s = snew_ref[0]
        m = m_ref[...]
        m_new = jnp.maximum(m, s)
        corr = jnp.exp(m - m_new)
        p = jnp.exp(s - m_new)
        l = l_ref[...] * corr + p
        acc = acc_ref[...] * corr + p * cnew_ref[0]
        o_ref[0] = acc / l


def _paged_attention(page_table, pool_ckv, pool_kpe, w_uk, qn, qpe, s_new, c_new, scale):
    b, n_pg = page_table.shape
    _, p, c = pool_ckv.shape
    r = pool_kpe.shape[2]
    h = qpe.shape[1]
    nps = PAGES_PER_STEP
    ind_t = (jnp.arange(w_uk.shape[1])[None, :] // (w_uk.shape[1] // h) == jnp.arange(h)[:, None]).astype(BF16)

    def page_spec(width, q):
        return pl.BlockSpec((1, p, width), lambda bi, pi, pt: (pt[bi * n_pg + pi * nps + q], 0, 0))

    in_specs = ([page_spec(c, q) for q in range(nps)] + [page_spec(r, q) for q in range(nps)] + [
        pl.BlockSpec(w_uk.shape, lambda bi, pi, pt: (0, 0)),
        pl.BlockSpec(ind_t.shape, lambda bi, pi, pt: (0, 0)),
        pl.BlockSpec((1, 1, qn.shape[2]), lambda bi, pi, pt: (bi, 0, 0)),
        pl.BlockSpec((1, h, r), lambda bi, pi, pt: (bi, 0, 0)),
        pl.BlockSpec((1, h, 1), lambda bi, pi, pt: (bi, 0, 0)),
        pl.BlockSpec((1, 1, c), lambda bi, pi, pt: (bi, 0, 0)),
    ])
    return pl.pallas_call(
        functools.partial(_paged_kernel, scale=scale, n_pages=nps),
        out_shape=jax.ShapeDtypeStruct((b, h, c), F32),
        grid_spec=pltpu.PrefetchScalarGridSpec(
            num_scalar_prefetch=1,
            grid=(b, n_pg // nps),
            in_specs=in_specs,
            out_specs=pl.BlockSpec((1, h, c), lambda bi, pi, pt: (bi, 0, 0)),
            scratch_shapes=[pltpu.VMEM((h, 1), F32), pltpu.VMEM((h, 1), F32), pltpu.VMEM((h, c), F32)]),
        compiler_params=_params("arbitrary", "arbitrary"),
        name="mla_sample_attention",
    )(page_table.reshape(-1), *([pool_ckv] * nps), *([pool_kpe] * nps), w_uk, ind_t, qn, qpe, s_new, c_new)


_NT = (((1,), (1,)), ((), ()))


def _split_bf16(x):
    hi = x.astype(BF16)
    return hi, (x - hi.astype(F32)).astype(BF16)


def _peer_score_kernel(x_ref, g_ref, wh_ref, wl_ref, kh_ref, kl_ref, xb_ref, s_ref, xl_ref, qt_ref):
    tt = x_ref.shape[0]
    n_heads = s_ref.shape[1]
    n_keys = s_ref.shape[2]
    g = g_ref[...]

    def norm_body(i, c):
        rows = pl.ds(pl.multiple_of(i * _NORM_ROWS, _NORM_ROWS), _NORM_ROWS)
        x = x_ref[rows, :]
        ms = jnp.mean(x * x, axis=-1, keepdims=True)
        hi, lo = _split_bf16((x * lax.rsqrt(ms + NORM_EPS)) * g)
        xb_ref[rows, :] = hi
        xl_ref[rows, :] = lo
        return c

    lax.fori_loop(0, tt // _NORM_ROWS, norm_body, 0)
    qt_ref[...] = lax.dot_general(wh_ref[...], xb_ref[...], _NT, preferred_element_type=F32)
    qt_ref[...] += lax.dot_general(wl_ref[...], xb_ref[...], _NT, preferred_element_type=F32)
    qt_ref[...] += lax.dot_general(wh_ref[...], xl_ref[...], _NT, preferred_element_type=F32)

    for half in range(2):
        def score_body(h, c, half=half):
            rows = pl.ds(pl.multiple_of((h * 2 + half) * n_keys, n_keys), n_keys)
            qh, ql = _split_bf16(qt_ref[rows, :])
            kh, kl = kh_ref[half], kl_ref[half]
            s_ref[half, h] = (jnp.dot(kh, qh, preferred_element_type=F32)
                              + jnp.dot(kl, qh, preferred_element_type=F32)
                              + jnp.dot(kh, ql, preferred_element_type=F32))
            return c

        lax.fori_loop(0, n_heads, score_body, 0)


def _peer_kernel(x_ref, xb_ref, s_ref, u_ref, vt_ref, o_ref,
                 e_ref, top_ref, thr_ref, act_ref, wact_ref, acc_ref):
    gi = pl.program_id(1)
    tt = x_ref.shape[0]
    n_heads = s_ref.shape[1]
    n_keys = s_ref.shape[2]
    n_lg = tt // LANES

    @pl.when(gi == 0)
    def _phase_a():
        acc_ref[...] = jnp.zeros(acc_ref.shape, F32)

        def top_body(i, c):
            lanes = pl.ds(pl.multiple_of((i // 2) * LANES, LANES), LANES)
            half = i % 2
            for h in range(n_heads):
                rem = s_ref[half, h, :, lanes]
                for a in range(PEER_TOPK):
                    mx = jnp.max(rem, axis=0, keepdims=True)
                    top_ref[half, a, h:h + 1, lanes] = mx
                    rem = jnp.where(rem == mx, -jnp.inf, rem)
            return c

        lax.fori_loop(0, 2 * n_lg, top_body, 0)

        for lg in range(n_lg):
            lanes = slice(lg * LANES, (lg + 1) * LANES)
            t1 = [top_ref[0, a, :, lanes] for a in range(PEER_TOPK)]
            t2 = [top_ref[1, a, :, lanes] for a in range(PEER_TOPK)]
            cands = [t1[a] + t2[b] for a, b in _CAND_PAIRS]
            rem = list(cands)
            thr = jnp.zeros_like(t1[0])
            cnt = jnp.zeros_like(t1[0])
            for _ in range(PEER_TOPK):
                mx = functools.reduce(jnp.maximum, rem)
                thr = jnp.where(cnt < PEER_TOPK, mx, thr)
                eq = [r == mx for r in rem]
                cnt = cnt + functools.reduce(jnp.add, [jnp.where(q, 1.0, 0.0) for q in eq])
                rem = [jnp.where(q, -jnp.inf, r) for q, r in zip(eq, rem)]
            e1 = [jnp.exp(t - t1[0]) for t in t1]
            e2 = [jnp.exp(t - t2[0]) for t in t2]
            z = functools.reduce(jnp.add, [jnp.where(cd >= thr, e1[a] * e2[b], 0.0)
                                           for cd, (a, b) in zip(cands, _CAND_PAIRS)])
            thr_ref[0, :, lanes] = thr
            thr_ref[1, :, lanes] = t1[0]
            thr_ref[2, :, lanes] = t2[0]
            thr_ref[3, :, lanes] = 1.0 / z

        def exp_body(i, c):
            lanes = pl.ds(pl.multiple_of(i * LANES, LANES), LANES)
            for h in range(n_heads):
                m1 = thr_ref[1, h:h + 1, lanes]
                m2 = thr_ref[2, h:h + 1, lanes]
                iz = thr_ref[3, h:h + 1, lanes]
                e_ref[0, h, :, lanes] = jnp.exp(s_ref[0, h, :, lanes] - m1) * iz
                e_ref[1, h, :, lanes] = jnp.exp(s_ref[1, h, :, lanes] - m2)
            return c

        lax.fori_loop(0, n_lg, exp_body, 0)

    act_ref[...] = lax.dot_general(u_ref[...], xb_ref[...], _NT, preferred_element_type=F32)
    assert u_ref.shape[0] == SUBLANES * n_keys
    tile_rows = pl.ds(pl.multiple_of(gi * SUBLANES, SUBLANES), SUBLANES)
    for q in range(SUBLANES):
        erows = slice(q * n_keys, (q + 1) * n_keys)
        for lg in range(n_lg):
            lanes = slice(lg * LANES, (lg + 1) * LANES)
            w = jnp.zeros((n_keys, LANES), F32)
            for h in range(n_heads):
                s1 = s_ref[0, h, tile_rows, lanes][q:q + 1]
                e1 = e_ref[0, h, tile_rows, lanes][q:q + 1]
                thr = thr_ref[0, h:h + 1, lanes]
                sel = (s1 + s_ref[1, h, :, lanes]) >= thr
                w = w + jnp.where(sel, e_ref[1, h, :, lanes], 0.0) * e1
            act = jax.nn.gelu(act_ref[erows, lanes])
            wact_ref[erows, lanes] = (w * act).astype(BF16)

    acc_ref[...] += jnp.dot(vt_ref[...], wact_ref[...], preferred_element_type=F32)

    @pl.when(gi == pl.num_programs(1) - 1)
    def _():
        for rb in range(acc_ref.shape[0] // LANES):
            for cb in range(n_lg):
                rs = slice(rb * LANES, (rb + 1) * LANES)
                cs = slice(cb * LANES, (cb + 1) * LANES)
                o_ref[cs, rs] = x_ref[cs, rs] + acc_ref[rs, cs].T


def _peer(x, g, w_q, sub_keys, u_tab, v_tab):
    m, d = x.shape
    n_exp = u_tab.shape[0]
    _, n_keys, dk = sub_keys.shape
    n_heads = w_q.shape[1] // (2 * dk)
    tt = TOKEN_TILE
    grp = PEER_GROUP
    wh, wl = _split_bf16(w_q.T)
    kh, kl = _split_bf16(sub_keys)
    const = lambda shape: pl.BlockSpec(shape, lambda i: (0,) * len(shape))
    xb, scores = pl.pallas_call(
        _peer_score_kernel,
        out_shape=(jax.ShapeDtypeStruct((m, d), BF16),
                   jax.ShapeDtypeStruct((2, n_heads, n_keys, m), F32)),
        grid=(m // tt,),
        in_specs=[pl.BlockSpec((tt, d), lambda i: (i, 0)), const((1, d)),
                  const(wh.shape), const(wl.shape), const(kh.shape), const(kl.shape)],
        out_specs=(pl.BlockSpec((tt, d), lambda i: (i, 0)),
                   pl.BlockSpec((2, n_heads, n_keys, tt), lambda i: (0, 0, 0, i))),
        scratch_shapes=[pltpu.VMEM((tt, d), BF16), pltpu.VMEM((wh.shape[0], tt), F32)],
        compiler_params=_params("arbitrary"),
        name="peer_scores",
    )(x, g.reshape(1, d), wh, wl, kh, kl)

    u = u_tab.astype(BF16)
    vt = v_tab.T.astype(BF16)
    return pl.pallas_call(
        _peer_kernel,
        out_shape=jax.ShapeDtypeStruct((m, d), F32),
        grid=(m // tt, n_exp // grp),
        in_specs=[pl.BlockSpec((tt, d), lambda i, gi: (i, 0)),
                  pl.BlockSpec((tt, d), lambda i, gi: (i, 0)),
                  pl.BlockSpec((2, n_heads, n_keys, tt), lambda i, gi: (0, 0, 0, i)),
                  pl.BlockSpec((grp, d), lambda i, gi: (gi, 0)),
                  pl.BlockSpec((d, grp), lambda i, gi: (0, gi))],
        out_specs=pl.BlockSpec((tt, d), lambda i, gi: (i, 0)),
        scratch_shapes=[
            pltpu.VMEM((2, n_heads, n_keys, tt), F32),
            pltpu.VMEM((2, PEER_TOPK, n_heads, tt), F32),
            pltpu.VMEM((4, n_heads, tt), F32),
            pltpu.VMEM((grp, tt), F32),
            pltpu.VMEM((grp, tt), BF16),
            pltpu.VMEM((d, tt), F32),
        ],
        compiler_params=_params("arbitrary", "arbitrary"),
        name="peer_dense",
    )(x, xb, scores, u, vt)


def _norm_last(x, g):
    return (x * lax.rsqrt(jnp.mean(x * x, axis=-1, keepdims=True) + NORM_EPS)) * g


def _rope(x, cos, sin):
    half = x.shape[-1] // 2
    x1, x2 = x[..., :half], x[..., half:]
    return jnp.concatenate([x1 * cos - x2 * sin, x2 * cos + x1 * sin], axis=-1)


def kernel(x_prompt, x_sample, cache_ckv, cache_kpe, page_table, state_wkv, state_shift, norm_mix, norm_ffn, rw_mu, rw_w_in, rw_w0, rw_w1, rw_w2, rw_a0, rw_a1, rw_a2, rw_v0, rw_v1, rw_v2, rw_g1, rw_g2, rw_k_k, rw_k_a, rw_r_k, rw_ln_w, rw_ln_b, rw_w_out, mla_w_in, mla_q_norm, mla_kv_norm, mla_w_qb, mla_w_kvb, mla_qn_nope, mla_qn_rope, mla_kn_nope, mla_kn_rope, mla_w_out, peer_w_q, peer_sub_keys, peer_u, peer_v):
    b, s, d = x_prompt.shape
    bs, ts, _ = x_sample.shape
    assert ts == 1
    n_p, n_s = b * s, bs * ts
    n_t = n_p + n_s
    n_pad = _round_up(n_t, TOKEN_TILE)
    depth = norm_mix.shape[0]
    rw_heads = d // RW_HEAD
    mla_heads = mla_w_qb.shape[2] // (MLA_NOPE + MLA_ROPE)
    kv_lora = mla_kv_norm.shape[1]
    q_lora = mla_q_norm.shape[1]
    mla_scale = (MLA_NOPE + MLA_ROPE) ** -0.5
    past_len = page_table.shape[1] * cache_ckv.shape[2]

    def merge_rows(p_rows, s_rows):
        pad = jnp.zeros((n_pad - n_t,) + p_rows.shape[1:], p_rows.dtype)
        return jnp.concatenate([p_rows, s_rows, pad], axis=0)

    x = merge_rows(x_prompt.reshape(n_p, d), x_sample.reshape(n_s, d))

    inv = ROPE_THETA ** (-jnp.arange(0, MLA_ROPE, 2, dtype=F32) / MLA_ROPE)
    pos = merge_rows(jnp.tile(jnp.arange(s), b), past_len + jnp.tile(jnp.arange(ts), bs)).astype(F32)
    ang = pos[:, None] * inv[None, :]
    cos, sin = jnp.cos(ang), jnp.sin(ang)

    def to_lanes(a, nb, nt):
        return a.reshape(nb, nt, rw_heads, RW_HEAD).transpose(1, 3, 0, 2).reshape(nt, RW_HEAD, nb * rw_heads)

    def from_lanes(a, nb, nt):
        return a.reshape(nt, RW_HEAD, nb, rw_heads).transpose(2, 0, 3, 1).reshape(nb * nt, d)

    v_first = None
    shifts_p, shifts_s, wkvs_p, wkvs_s = [], [], [], []
    ckvs_p, ckvs_s, kpes_p, kpes_s = [], [], [], []
    for i in range(depth):
        j = i // 2
        h = _rmsnorm(x, norm_mix[i])
        if i % 2 == 0:
            hp = h[:n_p].reshape(b, s, d)
            hs = h[n_p:n_t]
            prev_p = jnp.concatenate([jnp.zeros((b, 1, d), F32), hp[:, :-1]], axis=1).reshape(n_p, d)
            prev = merge_rows(prev_p, state_shift[j])
            xx = prev - h
            xr, xw, xk, xv, xa, xg = [h + xx * rw_mu[j, m] for m in range(6)]
            r = _mm(xr, rw_w_in[j, 0])
            k = _mm(xk, rw_w_in[j, 1])
            v = _mm(xv, rw_w_in[j, 2])
            w_lin = rw_w0[j] + _mm(jnp.tanh(_mm(xw, rw_w1[j])), rw_w2[j])
            decay = jnp.exp(-jnp.exp(-jax.nn.softplus(-w_lin) - 0.5))
            a = jax.nn.sigmoid(rw_a0[j] + _mm(_mm(xa, rw_a1[j]), rw_a2[j]))
            g = _mm(jax.nn.sigmoid(_mm(xg, rw_g1[j])), rw_g2[j])
            if j == 0:
                v_first = v
            else:
                v = v + (v_first - v) * jax.nn.sigmoid(rw_v0[j - 1] + _mm(_mm(xv, rw_v1[j - 1]), rw_v2[j - 1]))
            kk = (k * rw_k_k[j]).reshape(n_pad, rw_heads, RW_HEAD)
            kk = (kk * lax.rsqrt(jnp.maximum(jnp.sum(kk * kk, axis=-1, keepdims=True), 1e-24))).reshape(n_pad, d)
            k = k * (1.0 + (a - 1.0) * rw_k_a[j])
            ka = kk * a

            ops = (r, decay, k, kk, ka, v)
            y_p, st_p = _wkv(*[to_lanes(o[:n_p], b, s) for o in ops],
                             jnp.zeros((RW_HEAD, RW_HEAD, b * rw_heads), F32))
            st_s0 = state_wkv[j].transpose(2, 3, 0, 1).reshape(RW_HEAD, RW_HEAD, bs * rw_heads)
            y_s, st_s = _wkv(*[to_lanes(o[n_p:n_t], bs, ts) for o in ops], st_s0)
            y = merge_rows(from_lanes(y_p, b, s), from_lanes(y_s, bs, ts))
            wkvs_p.append(st_p.reshape(RW_HEAD, RW_HEAD, b, rw_heads).transpose(2, 3, 0, 1))
            wkvs_s.append(st_s.reshape(RW_HEAD, RW_HEAD, bs, rw_heads).transpose(2, 3, 0, 1))
            shifts_p.append(hp[:, -1])
            shifts_s.append(hs)

            yh = y.reshape(n_pad, rw_heads, RW_HEAD)
            yc = yh - jnp.mean(yh, axis=-1, keepdims=True)
            var = jnp.mean(yc * yc, axis=-1, keepdims=True)
            yn = (yc * lax.rsqrt(var + RW_GN_EPS)).reshape(n_pad, d) * rw_ln_w[j] + rw_ln_b[j]
            r_h = r.reshape(n_pad, rw_heads, RW_HEAD)
            bonus = jnp.sum(r_h * k.reshape(r_h.shape) * rw_r_k[j], axis=-1, keepdims=True) * v.reshape(r_h.shape)
            out = _mm((yn + bonus.reshape(n_pad, d)) * g, rw_w_out[j])
        else:
            proj = _mm(h, mla_w_in[j])
            cq = _norm_last(proj[:, :q_lora], mla_q_norm[j])
            ckv = _norm_last(proj[:, q_lora:q_lora + kv_lora], mla_kv_norm[j])
            k_pe = _rope(_norm_last(proj[:, q_lora + kv_lora:], mla_kn_rope[j]), cos, sin)
            q = _mm(cq, mla_w_qb[j]).reshape(n_pad, mla_heads, MLA_NOPE + MLA_ROPE)
            q_nope = _norm_last(q[..., :MLA_NOPE], mla_qn_nope[j])
            q_pe = _rope(_norm_last(q[..., MLA_NOPE:], mla_qn_rope[j]), cos[:, None], sin[:, None])
            w_kvb = mla_w_kvb[j].reshape(kv_lora, mla_heads, MLA_NOPE + MLA_V)
            w_uk, w_uv = w_kvb[..., :MLA_NOPE], w_kvb[..., MLA_NOPE:]
            kv_up = _mm(ckv, mla_w_kvb[j]).reshape(n_pad, mla_heads, MLA_NOPE + MLA_V)
            k_nope = _norm_last(kv_up[..., :MLA_NOPE], mla_kn_nope[j])
            ckvs_p.append(ckv[:n_p].reshape(b, s, kv_lora))
            ckvs_s.append(ckv[n_p:n_t].reshape(bs, ts, kv_lora))
            kpes_p.append(k_pe[:n_p].reshape(b, s, MLA_ROPE))
            kpes_s.append(k_pe[n_p:n_t].reshape(bs, ts, MLA_ROPE))

            def bhsd(t):
                return t[:n_p].reshape(b, s, mla_heads, t.shape[-1]).transpose(0, 2, 1, 3)

            zpad = jnp.zeros((b, mla_heads, s, LANES - MLA_NOPE - MLA_ROPE), BF16)
            qf = jnp.concatenate([bhsd(q_nope).astype(BF16), bhsd(q_pe).astype(BF16), zpad], axis=-1)
            kpe_b = jnp.broadcast_to(k_pe[:n_p].reshape(b, 1, s, MLA_ROPE), (b, mla_heads, s, MLA_ROPE))
            kf = jnp.concatenate([bhsd(k_nope).astype(BF16), kpe_b.astype(BF16), zpad], axis=-1)
            vf = bhsd(kv_up[..., MLA_NOPE:]).astype(BF16)
            o_p = _flash(qf, kf, vf, mla_scale).transpose(0, 2, 1, 3).reshape(n_p, mla_heads * MLA_V)

            qn_s = (q_nope[n_p:n_t] * mla_kn_nope[j]).reshape(n_s, 1, mla_heads * MLA_NOPE)
            qpe_s = q_pe[n_p:n_t]
            s_new = (jnp.sum(q_nope[n_p:n_t] * k_nope[n_p:n_t], axis=-1)
                     + jnp.sum(qpe_s * k_pe[n_p:n_t, None, :], axis=-1)) * mla_scale
            o_lat = _paged_attention(page_table, cache_ckv[j], cache_kpe[j],
                                     w_uk.reshape(kv_lora, mla_heads * MLA_NOPE).astype(BF16),
                                     qn_s, qpe_s, s_new[..., None], ckv[n_p:n_t].reshape(n_s, 1, kv_lora),
                                     mla_scale)
            o_s = jnp.einsum('bhc,chv->bhv', o_lat, w_uv, precision=lax.Precision.HIGHEST)
            out = _mm(merge_rows(o_p, o_s.reshape(n_s, mla_heads * MLA_V)), mla_w_out[j])
        x = x + out
        x = _peer(x, norm_ffn[i], peer_w_q[i], peer_sub_keys[i], peer_u[i], peer_v[i])

    y_prompt = x[:n_p].reshape(b, s, d)
    y_sample = x[n_p:n_t].reshape(bs, ts, d)
    return (y_prompt, y_sample, jnp.stack(ckvs_p), jnp.stack(kpes_p), jnp.stack(wkvs_p), jnp.stack(shifts_p),
            jnp.stack(ckvs_s), jnp.stack(kpes_s), jnp.stack(wkvs_s), jnp.stack(shifts_s))
```

```python
import functools

import jax
import jax.numpy as jnp
from jax import lax
from jax.experimental import pallas as pl
from jax.experimental.pallas import tpu as pltpu

F32 = jnp.float32
BF16 = jnp.bfloat16

LANES = 128
SUBLANES = 8
VMEM_LIMIT_BYTES = 52 * 1024 * 1024

RW_HEAD = 64
RW_GN_EPS = 64e-5
MLA_NOPE = 64
MLA_ROPE = 32
MLA_V = 64
ROPE_THETA = 10000.0
NORM_EPS = 1e-6
PEER_TOPK = 16

TOKEN_TILE = 512
PEER_GROUP = 1024
WKV_TOKENS = 32
FLASH_BLOCK = 256
FLASH_HEADS = 2
PAGES_PER_STEP = 4

_CAND_PAIRS = tuple((a, b) for a in range(PEER_TOPK) for b in range(PEER_TOPK)
                    if (a + 1) * (b + 1) <= PEER_TOPK)


_NT = (((1,), (1,)), ((), ()))


def _round_up(n, m):
    return (n + m - 1) // m * m


def _params(*sem):
    return pltpu.CompilerParams(dimension_semantics=sem, vmem_limit_bytes=VMEM_LIMIT_BYTES)


def _mm_kernel(x_ref, w_ref, o_ref):
    o_ref[...] = jnp.dot(x_ref[...].astype(BF16), w_ref[...], preferred_element_type=F32)


def _mm_res_kernel(x_ref, w_ref, r_ref, o_ref):
    o_ref[...] = r_ref[...] + jnp.dot(x_ref[...].astype(BF16), w_ref[...], preferred_element_type=F32)


def _mm(x, w, res=None):
    m, k = x.shape
    n = w.shape[1]
    tm = TOKEN_TILE if m % TOKEN_TILE == 0 else m
    row_spec = lambda width: pl.BlockSpec((tm, width), lambda i: (i, 0))
    in_specs = [row_spec(k), pl.BlockSpec((k, n), lambda i: (0, 0))]
    args = (x, w.astype(BF16))
    if res is not None:
        in_specs.append(row_spec(n))
        args += (res,)
    return pl.pallas_call(
        _mm_kernel if res is None else _mm_res_kernel,
        out_shape=jax.ShapeDtypeStruct((m, n), F32),
        grid=(m // tm,),
        in_specs=in_specs,
        out_specs=row_spec(n),
        compiler_params=_params("arbitrary"),
        name="token_matmul",
    )(*args)


_NORM_ROWS = 32


def _rmsnorm_kernel(x_ref, g_ref, o_ref):
    g = g_ref[...]

    def body(i, c):
        rows = pl.ds(pl.multiple_of(i * _NORM_ROWS, _NORM_ROWS), _NORM_ROWS)
        x = x_ref[rows, :]
        ms = jnp.mean(x * x, axis=-1, keepdims=True)
        o_ref[rows, :] = (x * lax.rsqrt(ms + NORM_EPS)) * g
        return c

    lax.fori_loop(0, x_ref.shape[0] // _NORM_ROWS, body, 0)


def _rmsnorm(x, g):
    m, d = x.shape
    tm = TOKEN_TILE if m % TOKEN_TILE == 0 else m
    return pl.pallas_call(
        _rmsnorm_kernel,
        out_shape=jax.ShapeDtypeStruct((m, d), F32),
        grid=(m // tm,),
        in_specs=[pl.BlockSpec((tm, d), lambda i: (i, 0)),
                  pl.BlockSpec((1, d), lambda i: (0, 0))],
        out_specs=pl.BlockSpec((tm, d), lambda i: (i, 0)),
        compiler_params=_params("arbitrary"),
        name="rmsnorm",
    )(x, g.reshape(1, d))


def _sublane_allsum(x):
    x = x + pltpu.roll(x, 4, 0)
    x = x + pltpu.roll(x, 2, 0)
    return x + pltpu.roll(x, 1, 0)


def _wkv_kernel(r_ref, w_ref, k_ref, kk_ref, ka_ref, v_ref, s0_ref, y_ref, s_ref):
    @pl.when(pl.program_id(1) == 0)
    def _():
        s_ref[...] = s0_ref[...]

    sub = lax.broadcasted_iota(jnp.int32, (SUBLANES, LANES), 0)

    def step(t, c):
        for vg in range(RW_HEAD // SUBLANES):
            yv = jnp.zeros((SUBLANES, LANES), F32)
            for vi in range(SUBLANES):
                vidx = vg * SUBLANES + vi
                s = s_ref[vidx]
                skk = _sublane_allsum(jnp.sum(s * kk_ref[t], axis=0))
                vrow = v_ref[t, pl.ds(vidx, 1), :]
                s = s * w_ref[t] - skk[None] * ka_ref[t] + vrow[None] * k_ref[t]
                s_ref[vidx] = s
                y = _sublane_allsum(jnp.sum(s * r_ref[t], axis=0))
                yv = jnp.where(sub == vi, y, yv)
            y_ref[t, vg * SUBLANES:(vg + 1) * SUBLANES, :] = yv
        return c

    lax.fori_loop(0, r_ref.shape[0], step, 0)


def _wkv(r, w, k, kk, ka, v, s0):
    t, n, l = r.shape
    tb = WKV_TOKENS if t % WKV_TOKENS == 0 else t
    nb = n // SUBLANES
    key4 = lambda a: a.reshape(t, nb, SUBLANES, l)
    kspec = pl.BlockSpec((tb, nb, SUBLANES, LANES), lambda g, i: (i, 0, 0, g))
    vspec = pl.BlockSpec((tb, n, LANES), lambda g, i: (i, 0, g))
    sspec = pl.BlockSpec((n, nb, SUBLANES, LANES), lambda g, i: (0, 0, 0, g))
    y, s = pl.pallas_call(
        _wkv_kernel,
        out_shape=(jax.ShapeDtypeStruct((t, n, l), F32),
                   jax.ShapeDtypeStruct((n, nb, SUBLANES, l), F32)),
        grid=(l // LANES, t // tb),
        in_specs=[kspec, kspec, kspec, kspec, kspec, vspec, sspec],
        out_specs=(vspec, sspec),
        compiler_params=_params("arbitrary", "arbitrary"),
        name="wkv7_scan",
    )(key4(r), key4(w), key4(k), key4(kk), key4(ka), v, s0.reshape(n, nb, SUBLANES, l))
    return y, s.reshape(n, n, l)


def _flash_kernel(q_ref, k_ref, v_ref, o_ref, *, scale):
    qi = pl.program_id(2)
    n_hd, tq = q_ref.shape[1], q_ref.shape[2]
    qs = [q_ref[0, h] for h in range(n_hd)]
    on_or_below_diagonal = (lax.broadcasted_iota(jnp.int32, (tq, tq), 1)
                            <= lax.broadcasted_iota(jnp.int32, (tq, tq), 0))

    def block(j, carry, diagonal):
        rows = pl.ds(pl.multiple_of(j * tq, tq), tq)
        out = []
        for h in range(n_hd):
            m, l, acc = carry[h]
            s = lax.dot_general(qs[h], k_ref[0, h, rows, :], _NT, preferred_element_type=F32) * scale
            if diagonal:
                s = jnp.where(on_or_below_diagonal, s, -jnp.inf)
            m_new = jnp.maximum(m, jnp.max(s, axis=-1, keepdims=True))
            p = jnp.exp(s - m_new)
            corr = jnp.exp(m - m_new)
            l = l * corr + jnp.sum(p, axis=-1, keepdims=True)
            acc = acc * corr + jnp.dot(p.astype(BF16), v_ref[0, h, rows, :], preferred_element_type=F32)
            out.append((m_new, l, acc))
        return tuple(out)

    init = tuple((jnp.full((tq, 1), -jnp.inf, F32), jnp.zeros((tq, 1), F32),
                  jnp.zeros((tq, v_ref.shape[3]), F32)) for _ in range(n_hd))
    carry = lax.fori_loop(0, qi, lambda j, c: block(j, c, False), init)
    carry = block(qi, carry, True)
    for h in range(n_hd):
        _, l, acc = carry[h]
        o_ref[0, h] = acc / l


def _flash(q, k, v, scale):
    b, h, s, dq = q.shape
    dv = v.shape[3]
    tq = FLASH_BLOCK
    hd = FLASH_HEADS
    return pl.pallas_call(
        functools.partial(_flash_kernel, scale=scale),
        out_shape=jax.ShapeDtypeStruct((b, h, s, dv), F32),
        grid=(b, h // hd, s // tq),
        in_specs=[pl.BlockSpec((1, hd, tq, dq), lambda bi, hi, qi: (bi, hi, qi, 0)),
                  pl.BlockSpec((1, hd, s, dq), lambda bi, hi, qi: (bi, hi, 0, 0)),
                  pl.BlockSpec((1, hd, s, dv), lambda bi, hi, qi: (bi, hi, 0, 0))],
        out_specs=pl.BlockSpec((1, hd, tq, dv), lambda bi, hi, qi: (bi, hi, qi, 0)),
        compiler_params=_params("arbitrary", "arbitrary", "arbitrary"),
        name="mla_prompt_attention",
    )(q, k, v)


def _paged_kernel(pt_ref, *refs, scale, n_pages):
    del pt_ref
    ckv_refs = refs[:n_pages]
    kpe_refs = refs[n_pages:2 * n_pages]
    (wuk_ref, ind_ref, qn_ref, qpet_ref, snew_ref, cnew_ref, o_ref,
     c_ref, kp_ref, qabs_ref, m_ref, l_ref, acc_ref) = refs[2 * n_pages:]
    pi = pl.program_id(1)
    n_h = o_ref.shape[1]
    p_sz = ckv_refs[0].shape[1]
    n_feat = wuk_ref.shape[1] // n_h

    @pl.when(pi == 0)
    def _():
        m_ref[...] = jnp.full(m_ref.shape, -jnp.inf, F32)
        l_ref[...] = jnp.zeros(l_ref.shape, F32)
        acc_ref[...] = jnp.zeros(acc_ref.shape, F32)
        wq = (wuk_ref[...].astype(F32) * qn_ref[0]).astype(BF16)
        qabs_ref[...] = jnp.dot(wq, ind_ref[...], preferred_element_type=F32).astype(BF16)

    for q, (ckv_ref, kpe_ref) in enumerate(zip(ckv_refs, kpe_refs)):
        c_ref[q * p_sz:(q + 1) * p_sz, :] = ckv_ref[0].astype(BF16)
        kp_ref[q * p_sz:(q + 1) * p_sz, :] = kpe_ref[0].astype(BF16)
    c = c_ref[...]
    kup = jnp.dot(c, wuk_ref[...], preferred_element_type=F32)
    ssq = jnp.dot((kup * kup).astype(BF16), ind_ref[...], preferred_element_type=F32)
    num = jnp.dot(c, qabs_ref[...], preferred_element_type=F32)
    s_pe = jnp.dot(kp_ref[...], qpet_ref[0], preferred_element_type=F32)
    s = (num * lax.rsqrt(ssq * (1.0 / n_feat) + NORM_EPS) + s_pe) * scale
    st = jnp.concatenate([s[i * LANES:(i + 1) * LANES].T[:n_h] for i in range(s.shape[0] // LANES)],
                         axis=1)
    m = m_ref[...]
    m_new = jnp.maximum(m, jnp.max(st, axis=-1, keepdims=True))
    corr = jnp.exp(m - m_new)
    p = jnp.exp(st - m_new)
    l_ref[...] = l_ref[...] * corr + jnp.sum(p, axis=-1, keepdims=True)
    acc_ref[...] = acc_ref[...] * corr + jnp.dot(p.astype(BF16), c, preferred_element_type=F32)
    m_ref[...] = m_new

    @pl.when(pi == pl.num_programs(1) - 1)
    def _():
        s = snew_ref[0]
        m = m_ref[...]
        m_new = jnp.maximum(m, s)
        corr = jnp.exp(m - m_new)
        p = jnp.exp(s - m_new)
        l = l_ref[...] * corr + p
        acc = acc_ref[...] * corr + p * cnew_ref[0]
        o_ref[0] = acc / l


def _paged_attention(page_table, pool_ckv, pool_kpe, w_uk, qn, qpe, s_new, c_new, scale):
    b, n_pg = page_table.shape
    _, p, c = pool_ckv.shape
    r = pool_kpe.shape[2]
    h = qpe.shape[1]
    nps = PAGES_PER_STEP
    ind = (jnp.arange(w_uk.shape[1])[:, None] // (w_uk.shape[1] // h) == jnp.arange(LANES)[None, :]).astype(BF16)
    qpe_t = jnp.pad(qpe.transpose(0, 2, 1), ((0, 0), (0, 0), (0, LANES - h))).astype(BF16)

    def page_spec(width, q):
        return pl.BlockSpec((1, p, width), lambda bi, pi, pt: (pt[bi * n_pg + pi * nps + q], 0, 0))

    in_specs = ([page_spec(c, q) for q in range(nps)] + [page_spec(r, q) for q in range(nps)] + [
        pl.BlockSpec(w_uk.shape, lambda bi, pi, pt: (0, 0)),
        pl.BlockSpec(ind.shape, lambda bi, pi, pt: (0, 0)),
        pl.BlockSpec((1, 1, qn.shape[2]), lambda bi, pi, pt: (bi, 0, 0)),
        pl.BlockSpec((1, r, LANES), lambda bi, pi, pt: (bi, 0, 0)),
        pl.BlockSpec((1, h, 1), lambda bi, pi, pt: (bi, 0, 0)),
        pl.BlockSpec((1, 1, c), lambda bi, pi, pt: (bi, 0, 0)),
    ])
    return pl.pallas_call(
        functools.partial(_paged_kernel, scale=scale, n_pages=nps),
        out_shape=jax.ShapeDtypeStruct((b, h, c), F32),
        grid_spec=pltpu.PrefetchScalarGridSpec(
            num_scalar_prefetch=1,
            grid=(b, n_pg // nps),
            in_specs=in_specs,
            out_specs=pl.BlockSpec((1, h, c), lambda bi, pi, pt: (bi, 0, 0)),
            scratch_shapes=[pltpu.VMEM((nps * p, c), BF16), pltpu.VMEM((nps * p, r), BF16),
                            pltpu.VMEM((c, LANES), BF16),
                            pltpu.VMEM((h, 1), F32), pltpu.VMEM((h, 1), F32), pltpu.VMEM((h, c), F32)]),
        compiler_params=_params("arbitrary", "arbitrary"),
        name="mla_sample_attention",
    )(page_table.reshape(-1), *([pool_ckv] * nps), *([pool_kpe] * nps), w_uk, ind, qn, qpe_t, s_new, c_new)


def _split_bf16(x):
    hi = x.astype(BF16)
    return hi, (x - hi.astype(F32)).astype(BF16)


def _peer_score_kernel(x_ref, g_ref, wh_ref, wl_ref, kh_ref, kl_ref, xb_ref, s_ref, xl_ref, qt_ref):
    tt = x_ref.shape[0]
    n_heads = s_ref.shape[1]
    n_keys = s_ref.shape[2]
    g = g_ref[...]

    def norm_body(i, c):
        rows = pl.ds(pl.multiple_of(i * _NORM_ROWS, _NORM_ROWS), _NORM_ROWS)
        x = x_ref[rows, :]
        ms = jnp.mean(x * x, axis=-1, keepdims=True)
        hi, lo = _split_bf16((x * lax.rsqrt(ms + NORM_EPS)) * g)
        xb_ref[rows, :] = hi
        xl_ref[rows, :] = lo
        return c

    lax.fori_loop(0, tt // _NORM_ROWS, norm_body, 0)
    qt_ref[...] = lax.dot_general(wh_ref[...], xb_ref[...], _NT, preferred_element_type=F32)
    qt_ref[...] += lax.dot_general(wl_ref[...], xb_ref[...], _NT, preferred_element_type=F32)
    qt_ref[...] += lax.dot_general(wh_ref[...], xl_ref[...], _NT, preferred_element_type=F32)

    for half in range(2):
        def score_body(h, c, half=half):
            rows = pl.ds(pl.multiple_of((h * 2 + half) * n_keys, n_keys), n_keys)
            qh, ql = _split_bf16(qt_ref[rows, :])
            kh, kl = kh_ref[half], kl_ref[half]
            s_ref[half, h] = (jnp.dot(kh, qh, preferred_element_type=F32)
                              + jnp.dot(kl, qh, preferred_element_type=F32)
                              + jnp.dot(kh, ql, preferred_element_type=F32))
            return c

        lax.fori_loop(0, n_heads, score_body, 0)


def _peer_kernel(x_ref, xb_ref, s_ref, u_ref, vt_ref, o_ref,
                 e1_ref, cnt_ref, e2_ref, r2_ref, top_ref, thr_ref, act_ref, wact_ref, acc_ref):
    gi = pl.program_id(1)
    tt = x_ref.shape[0]
    n_heads = s_ref.shape[1]
    n_keys = s_ref.shape[2]
    n_lg = tt // LANES
    not_ranked = float(n_keys - 1)

    @pl.when(gi == 0)
    def _phase_a():
        acc_ref[...] = jnp.zeros(acc_ref.shape, F32)

        def top_body(i, c):
            lanes = pl.ds(pl.multiple_of(i * LANES, LANES), LANES)
            for half in range(2):
                for h in range(n_heads):
                    rem = s_ref[half, h, :, lanes]
                    rank = jnp.full(rem.shape, not_ranked, F32)
                    for a in range(PEER_TOPK):
                        mx = jnp.max(rem, axis=0, keepdims=True)
                        top_ref[half, a, h:h + 1, lanes] = mx
                        eq = rem == mx
                        rem = jnp.where(eq, -jnp.inf, rem)
                        if half == 1:
                            rank = jnp.where(eq, float(a), rank)
                    if half == 1:
                        r2_ref[h, :, lanes] = rank.astype(BF16)
            return c

        lax.fori_loop(0, n_lg, top_body, 0)

        for lg in range(n_lg):
            lanes = slice(lg * LANES, (lg + 1) * LANES)
            t1 = [top_ref[0, a, :, lanes] for a in range(PEER_TOPK)]
            t2 = [top_ref[1, a, :, lanes] for a in range(PEER_TOPK)]
            cands = [t1[a] + t2[b] for a, b in _CAND_PAIRS]
            rem = list(cands)
            thr = jnp.zeros_like(t1[0])
            cnt = jnp.zeros_like(t1[0])
            for _ in range(PEER_TOPK):
                mx = functools.reduce(jnp.maximum, rem)
                thr = jnp.where(cnt < PEER_TOPK, mx, thr)
                eq = [r == mx for r in rem]
                cnt = cnt + functools.reduce(jnp.add, [jnp.where(q, 1.0, 0.0) for q in eq])
                rem = [jnp.where(q, -jnp.inf, r) for q, r in zip(eq, rem)]
            e1 = [jnp.exp(t - t1[0]) for t in t1]
            e2 = [jnp.exp(t - t2[0]) for t in t2]
            z = functools.reduce(jnp.add, [jnp.where(cd >= thr, e1[a] * e2[b], 0.0)
                                           for cd, (a, b) in zip(cands, _CAND_PAIRS)])
            thr_ref[0, :, lanes] = thr
            thr_ref[1, :, lanes] = t1[0]
            thr_ref[2, :, lanes] = t2[0]
            thr_ref[3, :, lanes] = 1.0 / z

        def exp_body(i, c):
            lanes = pl.ds(pl.multiple_of(i * LANES, LANES), LANES)
            for h in range(n_heads):
                thr = thr_ref[0, h:h + 1, lanes]
                m1 = thr_ref[1, h:h + 1, lanes]
                m2 = thr_ref[2, h:h + 1, lanes]
                iz = thr_ref[3, h:h + 1, lanes]
                s1 = s_ref[0, h, :, lanes]
                e1_ref[h, :, lanes] = jnp.exp(s1 - m1) * iz
                e2_ref[h, :, lanes] = jnp.exp(s_ref[1, h, :, lanes] - m2).astype(BF16)
                cnt = jnp.zeros(s1.shape, F32)
                for b in range(PEER_TOPK):
                    cnt = cnt + jnp.where(s1 + top_ref[1, b, h:h + 1, lanes] >= thr, 1.0, 0.0)
                cnt_ref[h, :, lanes] = cnt
            return c

        lax.fori_loop(0, n_lg, exp_body, 0)

    act_ref[...] = lax.dot_general(u_ref[...], xb_ref[...], _NT, preferred_element_type=F32)
    assert u_ref.shape[0] == SUBLANES * n_keys
    tile_rows = pl.ds(pl.multiple_of(gi * SUBLANES, SUBLANES), SUBLANES)
    for q in range(SUBLANES):
        erows = slice(q * n_keys, (q + 1) * n_keys)
        for lg in range(n_lg):
            lanes = slice(lg * LANES, (lg + 1) * LANES)
            w = jnp.zeros((n_keys, LANES), BF16)
            for h in range(n_heads):
                cnt = cnt_ref[h, tile_rows, lanes][q:q + 1].astype(BF16)
                e1 = e1_ref[h, tile_rows, lanes][q:q + 1].astype(BF16)
                e2 = e2_ref[h, :, lanes]
                w = w + jnp.where(r2_ref[h, :, lanes] < cnt, e2, jnp.zeros_like(e2)) * e1
            act = jax.nn.gelu(act_ref[erows, lanes]).astype(BF16)
            wact_ref[erows, lanes] = w * act

    acc_ref[...] += jnp.dot(vt_ref[...], wact_ref[...], preferred_element_type=F32)

    @pl.when(gi == pl.num_programs(1) - 1)
    def _():
        for rb in range(acc_ref.shape[0] // LANES):
            for cb in range(n_lg):
                rs = slice(rb * LANES, (rb + 1) * LANES)
                cs = slice(cb * LANES, (cb + 1) * LANES)
                o_ref[cs, rs] = x_ref[cs, rs] + acc_ref[rs, cs].T


def _peer(x, g, w_q, sub_keys, u_tab, v_tab):
    m, d = x.shape
    n_exp = u_tab.shape[0]
    _, n_keys, dk = sub_keys.shape
    n_heads = w_q.shape[1] // (2 * dk)
    tt = TOKEN_TILE
    grp = PEER_GROUP
    wh, wl = _split_bf16(w_q.T)
    kh, kl = _split_bf16(sub_keys)
    const = lambda shape: pl.BlockSpec(shape, lambda i: (0,) * len(shape))
    xb, scores = pl.pallas_call(
        _peer_score_kernel,
        out_shape=(jax.ShapeDtypeStruct((m, d), BF16),
                   jax.ShapeDtypeStruct((2, n_heads, n_keys, m), F32)),
        grid=(m // tt,),
        in_specs=[pl.BlockSpec((tt, d), lambda i: (i, 0)), const((1, d)),
                  const(wh.shape), const(wl.shape), const(kh.shape), const(kl.shape)],
        out_specs=(pl.BlockSpec((tt, d), lambda i: (i, 0)),
                   pl.BlockSpec((2, n_heads, n_keys, tt), lambda i: (0, 0, 0, i))),
        scratch_shapes=[pltpu.VMEM((tt, d), BF16), pltpu.VMEM((wh.shape[0], tt), F32)],
        compiler_params=_params("arbitrary"),
        name="peer_scores",
    )(x, g.reshape(1, d), wh, wl, kh, kl)

    u = u_tab.astype(BF16)
    vt = v_tab.T.astype(BF16)
    return pl.pallas_call(
        _peer_kernel,
        out_shape=jax.ShapeDtypeStruct((m, d), F32),
        grid=(m // tt, n_exp // grp),
        in_specs=[pl.BlockSpec((tt, d), lambda i, gi: (i, 0)),
                  pl.BlockSpec((tt, d), lambda i, gi: (i, 0)),
                  pl.BlockSpec((2, n_heads, n_keys, tt), lambda i, gi: (0, 0, 0, i)),
                  pl.BlockSpec((grp, d), lambda i, gi: (gi, 0)),
                  pl.BlockSpec((d, grp), lambda i, gi: (0, gi))],
        out_specs=pl.BlockSpec((tt, d), lambda i, gi: (i, 0)),
        scratch_shapes=[
            pltpu.VMEM((n_heads, n_keys, tt), F32),
            pltpu.VMEM((n_heads, n_keys, tt), F32),
            pltpu.VMEM((n_heads, n_keys, tt), BF16),
            pltpu.VMEM((n_heads, n_keys, tt), BF16),
            pltpu.VMEM((2, PEER_TOPK, n_heads, tt), F32),
            pltpu.VMEM((4, n_heads, tt), F32),
            pltpu.VMEM((grp, tt), F32),
            pltpu.VMEM((grp, tt), BF16),
            pltpu.VMEM((d, tt), F32),
        ],
        compiler_params=_params("arbitrary", "arbitrary"),
        name="peer_dense",
    )(x, xb, scores, u, vt)


def _norm_last(x, g):
    return (x * lax.rsqrt(jnp.mean(x * x, axis=-1, keepdims=True) + NORM_EPS)) * g


def _rope(x, cos, sin):
    half = x.shape[-1] // 2
    x1, x2 = x[..., :half], x[..., half:]
    return jnp.concatenate([x1 * cos - x2 * sin, x2 * cos + x1 * sin], axis=-1)


def kernel(x_prompt, x_sample, cache_ckv, cache_kpe, page_table, state_wkv, state_shift, norm_mix, norm_ffn, rw_mu, rw_w_in, rw_w0, rw_w1, rw_w2, rw_a0, rw_a1, rw_a2, rw_v0, rw_v1, rw_v2, rw_g1, rw_g2, rw_k_k, rw_k_a, rw_r_k, rw_ln_w, rw_ln_b, rw_w_out, mla_w_in, mla_q_norm, mla_kv_norm, mla_w_qb, mla_w_kvb, mla_qn_nope, mla_qn_rope, mla_kn_nope, mla_kn_rope, mla_w_out, peer_w_q, peer_sub_keys, peer_u, peer_v):
    b, s, d = x_prompt.shape
    bs, ts, _ = x_sample.shape
    assert ts == 1
    n_p, n_s = b * s, bs * ts
    n_t = n_p + n_s
    n_pad = _round_up(n_t, TOKEN_TILE)
    depth = norm_mix.shape[0]
    rw_heads = d // RW_HEAD
    mla_heads = mla_w_qb.shape[2] // (MLA_NOPE + MLA_ROPE)
    kv_lora = mla_kv_norm.shape[1]
    q_lora = mla_q_norm.shape[1]
    mla_scale = (MLA_NOPE + MLA_ROPE) ** -0.5
    past_len = page_table.shape[1] * cache_ckv.shape[2]

    def merge_rows(p_rows, s_rows):
        pad = jnp.zeros((n_pad - n_t,) + p_rows.shape[1:], p_rows.dtype)
        return jnp.concatenate([p_rows, s_rows, pad], axis=0)

    x = merge_rows(x_prompt.reshape(n_p, d), x_sample.reshape(n_s, d))

    inv = ROPE_THETA ** (-jnp.arange(0, MLA_ROPE, 2, dtype=F32) / MLA_ROPE)
    pos = merge_rows(jnp.tile(jnp.arange(s), b), past_len + jnp.tile(jnp.arange(ts), bs)).astype(F32)
    ang = pos[:, None] * inv[None, :]
    cos, sin = jnp.cos(ang), jnp.sin(ang)

    def to_lanes(a, nb, nt):
        return a.reshape(nb, nt, rw_heads, RW_HEAD).transpose(1, 3, 0, 2).reshape(nt, RW_HEAD, nb * rw_heads)

    def from_lanes(a, nb, nt):
        return a.reshape(nt, RW_HEAD, nb, rw_heads).transpose(2, 0, 3, 1).reshape(nb * nt, d)

    v_first = None
    shifts_p, shifts_s, wkvs_p, wkvs_s = [], [], [], []
    ckvs_p, ckvs_s, kpes_p, kpes_s = [], [], [], []
    for i in range(depth):
        j = i // 2
        h = _rmsnorm(x, norm_mix[i])
        if i % 2 == 0:
            hp = h[:n_p].reshape(b, s, d)
            hs = h[n_p:n_t]
            prev_p = jnp.concatenate([jnp.zeros((b, 1, d), F32), hp[:, :-1]], axis=1).reshape(n_p, d)
            prev = merge_rows(prev_p, state_shift[j])
            xx = prev - h
            xr, xw, xk, xv, xa, xg = [h + xx * rw_mu[j, m] for m in range(6)]
            r = _mm(xr, rw_w_in[j, 0])
            k = _mm(xk, rw_w_in[j, 1])
            v = _mm(xv, rw_w_in[j, 2])
            w_lin = rw_w0[j] + _mm(jnp.tanh(_mm(xw, rw_w1[j])), rw_w2[j])
            decay = jnp.exp(-jnp.exp(-jax.nn.softplus(-w_lin) - 0.5))
            a = jax.nn.sigmoid(rw_a0[j] + _mm(_mm(xa, rw_a1[j]), rw_a2[j]))
            g = _mm(jax.nn.sigmoid(_mm(xg, rw_g1[j])), rw_g2[j])
            if j == 0:
                v_first = v
            else:
                v = v + (v_first - v) * jax.nn.sigmoid(rw_v0[j - 1] + _mm(_mm(xv, rw_v1[j - 1]), rw_v2[j - 1]))
            kk = (k * rw_k_k[j]).reshape(n_pad, rw_heads, RW_HEAD)
            kk = (kk * lax.rsqrt(jnp.maximum(jnp.sum(kk * kk, axis=-1, keepdims=True), 1e-24))).reshape(n_pad, d)
            k = k * (1.0 + (a - 1.0) * rw_k_a[j])
            ka = kk * a

            ops = (r, decay, k, kk, ka, v)
            y_p, st_p = _wkv(*[to_lanes(o[:n_p], b, s) for o in ops],
                             jnp.zeros((RW_HEAD, RW_HEAD, b * rw_heads), F32))
            st_s0 = state_wkv[j].transpose(2, 3, 0, 1).reshape(RW_HEAD, RW_HEAD, bs * rw_heads)
            y_s, st_s = _wkv(*[to_lanes(o[n_p:n_t], bs, ts) for o in ops], st_s0)
            y = merge_rows(from_lanes(y_p, b, s), from_lanes(y_s, bs, ts))
            wkvs_p.append(st_p.reshape(RW_HEAD, RW_HEAD, b, rw_heads).transpose(2, 3, 0, 1))
            wkvs_s.append(st_s.reshape(RW_HEAD, RW_HEAD, bs, rw_heads).transpose(2, 3, 0, 1))
            shifts_p.append(hp[:, -1])
            shifts_s.append(hs)

            yh = y.reshape(n_pad, rw_heads, RW_HEAD)
            yc = yh - jnp.mean(yh, axis=-1, keepdims=True)
            var = jnp.mean(yc * yc, axis=-1, keepdims=True)
            yn = (yc * lax.rsqrt(var + RW_GN_EPS)).reshape(n_pad, d) * rw_ln_w[j] + rw_ln_b[j]
            r_h = r.reshape(n_pad, rw_heads, RW_HEAD)
            bonus = jnp.sum(r_h * k.reshape(r_h.shape) * rw_r_k[j], axis=-1, keepdims=True) * v.reshape(r_h.shape)
            x = _mm((yn + bonus.reshape(n_pad, d)) * g, rw_w_out[j], res=x)
        else:
            proj = _mm(h, mla_w_in[j])
            cq = _norm_last(proj[:, :q_lora], mla_q_norm[j])
            ckv = _norm_last(proj[:, q_lora:q_lora + kv_lora], mla_kv_norm[j])
            k_pe = _rope(_norm_last(proj[:, q_lora + kv_lora:], mla_kn_rope[j]), cos, sin)
            q = _mm(cq, mla_w_qb[j]).reshape(n_pad, mla_heads, MLA_NOPE + MLA_ROPE)
            q_nope = _norm_last(q[..., :MLA_NOPE], mla_qn_nope[j])
            q_pe = _rope(_norm_last(q[..., MLA_NOPE:], mla_qn_rope[j]), cos[:, None], sin[:, None])
            w_kvb = mla_w_kvb[j].reshape(kv_lora, mla_heads, MLA_NOPE + MLA_V)
            w_uk, w_uv = w_kvb[..., :MLA_NOPE], w_kvb[..., MLA_NOPE:]
            kv_up = _mm(ckv, mla_w_kvb[j]).reshape(n_pad, mla_heads, MLA_NOPE + MLA_V)
            k_nope = _norm_last(kv_up[..., :MLA_NOPE], mla_kn_nope[j])
            ckvs_p.append(ckv[:n_p].reshape(b, s, kv_lora))
            ckvs_s.append(ckv[n_p:n_t].reshape(bs, ts, kv_lora))
            kpes_p.append(k_pe[:n_p].reshape(b, s, MLA_ROPE))
            kpes_s.append(k_pe[n_p:n_t].reshape(bs, ts, MLA_ROPE))

            def bhsd(t):
                return t[:n_p].reshape(b, s, mla_heads, t.shape[-1]).transpose(0, 2, 1, 3)

            zpad = jnp.zeros((b, mla_heads, s, LANES - MLA_NOPE - MLA_ROPE), BF16)
            qf = jnp.concatenate([bhsd(q_nope).astype(BF16), bhsd(q_pe).astype(BF16), zpad], axis=-1)
            kpe_b = jnp.broadcast_to(k_pe[:n_p].reshape(b, 1, s, MLA_ROPE), (b, mla_heads, s, MLA_ROPE))
            kf = jnp.concatenate([bhsd(k_nope).astype(BF16), kpe_b.astype(BF16), zpad], axis=-1)
            vf = bhsd(kv_up[..., MLA_NOPE:]).astype(BF16)
            o_p = _flash(qf, kf, vf, mla_scale).transpose(0, 2, 1, 3).reshape(n_p, mla_heads * MLA_V)

            qn_s = (q_nope[n_p:n_t] * mla_kn_nope[j]).reshape(n_s, 1, mla_heads * MLA_NOPE)
            qpe_s = q_pe[n_p:n_t]
            s_new = (jnp.sum(q_nope[n_p:n_t] * k_nope[n_p:n_t], axis=-1)
                     + jnp.sum(qpe_s * k_pe[n_p:n_t, None, :], axis=-1)) * mla_scale
            o_lat = _paged_attention(page_table, cache_ckv[j], cache_kpe[j],
                                     w_uk.reshape(kv_lora, mla_heads * MLA_NOPE).astype(BF16),
                                     qn_s, qpe_s, s_new[..., None], ckv[n_p:n_t].reshape(n_s, 1, kv_lora),
                                     mla_scale)
            o_s = jnp.einsum('bhc,chv->bhv', o_lat, w_uv, precision=lax.Precision.HIGHEST)
            x = _mm(merge_rows(o_p, o_s.reshape(n_s, mla_heads * MLA_V)), mla_w_out[j], res=x)
        x = _peer(x, norm_ffn[i], peer_w_q[i], peer_sub_keys[i], peer_u[i], peer_v[i])

    y_prompt = x[:n_p].reshape(b, s, d)
    y_sample = x[n_p:n_t].reshape(bs, ts, d)
    return (y_prompt, y_sample, jnp.stack(ckvs_p), jnp.stack(kpes_p), jnp.stack(wkvs_p), jnp.stack(shifts_p),
            jnp.stack(ckvs_s), jnp.stack(kpes_s), jnp.stack(wkvs_s), jnp.stack(shifts_s))
```
